```python
import math, functools
import jax, jax.numpy as jnp
from jax import lax
import numpy as np

D_MODEL = 2048
BATCH = 2
SEQ = 4096
DEPTH = 4
DEC_BATCH = 8
DEC_SEQ = 1
PAST_LEN = 16384
PAGE_SIZE = 128

POOL_WINDOWS = (2, 4, 8, 16)
N_POOL_GROUPS = len(POOL_WINDOWS)
POOL_WIDTH = D_MODEL
POOL_GROUP = POOL_WIDTH // N_POOL_GROUPS
POOL_BUF = max(POOL_WINDOWS) - 1
DA_HEADS = D_MODEL // 256
DA_QK_DIM = 128
DA_V_DIM = 2 * DA_QK_DIM
DA_WIDTH = DA_HEADS * DA_V_DIM
Q_BLOCK = 128
RET_HEADS = D_MODEL // 256
RET_QK_DIM = 256
RET_V_DIM = 256
RET_WIDTH = RET_HEADS * RET_V_DIM
RET_CHUNK = 128
ROPE_BASE = 10000.0
D_FF = -(-8 * D_MODEL // (3 * 256)) * 256
EPS = 1e-6
IN_SPLITS = (POOL_WIDTH, DA_HEADS * 2 * DA_QK_DIM, DA_HEADS * 2 * DA_QK_DIM, DA_WIDTH,
             RET_HEADS * RET_QK_DIM, RET_HEADS * RET_QK_DIM, RET_WIDTH, RET_WIDTH, 3 * D_MODEL)
IN_COLS = sum(IN_SPLITS)

kernel_name = "hybrid_pool_diffattn_retention_step"


def rmsnorm(x, g):
    xf = x.astype(jnp.float32)
    y = xf * lax.rsqrt(jnp.mean(xf * xf, axis=-1, keepdims=True) + EPS)
    return (y * g.astype(jnp.float32)).astype(x.dtype)


def split_proj(p):
    parts, off = [], 0
    for n in IN_SPLITS:
        parts.append(p[..., off:off + n])
        off += n
    return parts


def rope(x, pos):
    half = x.shape[-1] // 2
    inv = ROPE_BASE ** (-jnp.arange(half, dtype=jnp.float32) / half)
    ang = pos.astype(jnp.float32)[:, None] * inv[None, :]
    cos = jnp.cos(ang)[None, :, None, :]
    sin = jnp.sin(ang)[None, :, None, :]
    xf = x.astype(jnp.float32)
    x1, x2 = xf[..., :half], xf[..., half:]
    return jnp.concatenate([x1 * cos - x2 * sin, x1 * sin + x2 * cos], axis=-1).astype(x.dtype)


def pool_mix(u_ext, pos, w_pool, pool_scale):
    T = u_ext.shape[1] - POOL_BUF
    uf = u_ext.astype(jnp.float32)
    cs = jnp.concatenate([jnp.zeros_like(uf[:, :1]), jnp.cumsum(uf, axis=1)], axis=1)
    u_new = uf[:, POOL_BUF:]
    outs = []
    for g, w in enumerate(POOL_WINDOWS):
        sl = slice(g * POOL_GROUP, (g + 1) * POOL_GROUP)
        s = cs[:, POOL_BUF + 1:POOL_BUF + 1 + T, sl] - cs[:, POOL_BUF + 1 - w:POOL_BUF + 1 - w + T, sl]
        cnt = jnp.minimum(w, pos + 1).astype(jnp.float32)[None, :, None]
        outs.append(s / cnt - u_new[..., sl])
    d = jnp.stack(outs, axis=2)
    y = jnp.einsum('btgc,gcd->btgd', d, w_pool.astype(jnp.float32)).reshape(d.shape[0], T, POOL_WIDTH)
    return (y * pool_scale.astype(jnp.float32)).astype(u_ext.dtype)


def diff_core(q1, q2, k1, k2, v, mask, lam):
    scale = q1.shape[-1] ** -0.5

    def smax(q, k):
        s = jnp.einsum('bqhd,bkhd->bhqk', q, k).astype(jnp.float32) * scale
        return jax.nn.softmax(jnp.where(mask, s, -jnp.inf), axis=-1)

    a = smax(q1, k1) - lam * smax(q2, k2)
    return jnp.einsum('bhqk,bkhe->bqhe', a.astype(v.dtype), v)


def attend_prompt(q1, q2, k_rows, v_rows, lam):
    B, S, H, _ = q1.shape
    k1, k2 = k_rows[..., :DA_QK_DIM], k_rows[..., DA_QK_DIM:]
    kpos = jnp.arange(S)

    def block(start):
        qb1 = lax.dynamic_slice_in_dim(q1, start, Q_BLOCK, axis=1)
        qb2 = lax.dynamic_slice_in_dim(q2, start, Q_BLOCK, axis=1)
        mask = kpos[None, :] <= (start + jnp.arange(Q_BLOCK))[:, None]
        return diff_core(qb1, qb2, k1, k2, v_rows, mask, lam)

    o = lax.map(block, jnp.arange(0, S, Q_BLOCK))
    return o.transpose(1, 0, 2, 3, 4).reshape(B, S, H, DA_V_DIM)


def attend_sample(q1, q2, k_rows, v_rows, lam, k_past, v_past):
    past = k_past.shape[1]
    T = q1.shape[1]
    k_all = jnp.concatenate([k_past.astype(k_rows.dtype), k_rows], axis=1)
    v_all = jnp.concatenate([v_past.astype(v_rows.dtype), v_rows], axis=1)
    mask = jnp.arange(past + T)[None, :] <= (past + jnp.arange(T))[:, None]
    return diff_core(q1, q2, k_all[..., :DA_QK_DIM], k_all[..., DA_QK_DIM:], v_all, mask, lam)


def retention(q, k, v, s0, chunk):
    B, T, H, dk = q.shape
    dv = v.shape[-1]
    n = T // chunk
    f32 = jnp.float32
    log_g = jnp.log1p(-jnp.exp2(-5.0 - jnp.arange(H, dtype=f32)))
    i = jnp.arange(chunk, dtype=f32)
    diff = i[:, None] - i[None, :]
    causal = diff >= 0
    dec = jnp.where(causal[None], jnp.exp(log_g[:, None, None] * jnp.where(causal, diff, 0.0)[None]), 0.0)
    q_dec = jnp.exp(log_g[None, :] * (i[:, None] + 1.0))[None, :, :, None]
    k_dec = jnp.exp(log_g[None, :] * (chunk - 1.0 - i[:, None]))[None, :, :, None]
    c_dec = jnp.exp(log_g * chunk)[None, :, None, None]

    def to_chunks(a):
        return a.astype(f32).reshape(B, n, chunk, H, a.shape[-1]).transpose(1, 0, 2, 3, 4)

    def step(s, inp):
        qb, kb, vb = inp
        sc = jnp.einsum('bihd,bjhd->bhij', qb, kb) * dec[None]
        o = jnp.einsum('bhij,bjhe->bihe', sc, vb) + jnp.einsum('bihd,bhde->bihe', qb, s) * q_dec
        s = s * c_dec + jnp.einsum('bjhd,bjhe->bhde', kb * k_dec, vb)
        return s, o

    s_fin, o = lax.scan(step, s0.astype(f32), (to_chunks(q), to_chunks(k), to_chunks(v)))
    o = o.transpose(1, 0, 2, 3, 4).reshape(B, T, H, dv)
    return o.astype(v.dtype), s_fin


def trunk_layer(x, pos, pool_prefix, ret_s0, ret_chunk, attend, lam, lam_init, lp):
    (g_mix_pre, g_mix_post, g_ffn_pre, g_ffn_post, w_in, w_pool, pool_scale, da_subln, ret_norm,
     w_br_a, w_br_b, w_br_c, w_o, w_ffn_in, w_ffn_out) = lp
    B, T, _ = x.shape
    h = rmsnorm(x, g_mix_pre)
    u, dq, dk, dv, rq, rk, rv, rg, gates = split_proj(h @ w_in)
    u_ext = jnp.concatenate([pool_prefix.astype(u.dtype), u], axis=1)
    ya = pool_mix(u_ext, pos, w_pool, pool_scale)
    pool_new = u_ext[:, -POOL_BUF:]
    dq = dq.reshape(B, T, DA_HEADS, 2 * DA_QK_DIM)
    k_rows = dk.reshape(B, T, DA_HEADS, 2 * DA_QK_DIM)
    v_rows = dv.reshape(B, T, DA_HEADS, DA_V_DIM)
    ob = attend(dq[..., :DA_QK_DIM], dq[..., DA_QK_DIM:], k_rows, v_rows, lam)
    yb = (rmsnorm(ob, da_subln) * (1.0 - lam_init)).reshape(B, T, DA_WIDTH)
    rq = rope(rq.reshape(B, T, RET_HEADS, RET_QK_DIM), pos)
    rk = rope(rk.reshape(B, T, RET_HEADS, RET_QK_DIM), pos) * (RET_QK_DIM ** -0.5)
    oc, s_new = retention(rq, rk, rv.reshape(B, T, RET_HEADS, RET_V_DIM), ret_s0, ret_chunk)
    oc = rmsnorm(oc, ret_norm.reshape(RET_HEADS, RET_V_DIM)).reshape(B, T, RET_WIDTH)
    yc = jax.nn.silu(rg) * oc
    ga, gb, gc = jnp.split(jax.nn.sigmoid(gates), 3, axis=-1)
    merged = ga * (ya @ w_br_a) + gb * (yb @ w_br_b) + gc * (yc @ w_br_c)
    x = x + rmsnorm(merged @ w_o, g_mix_post)
    a, b = jnp.split(rmsnorm(x, g_ffn_pre) @ w_ffn_in, 2, axis=-1)
    x = x + rmsnorm((jax.nn.silu(a) * b) @ w_ffn_out, g_ffn_post)
    return x, pool_new, k_rows, v_rows, s_new


def setup_inputs(seed: int = 0) -> dict:
    key = jax.random.key(seed)
    ks = iter(jax.random.split(key, 40))
    f32 = jnp.float32

    def nrm(shape, scale):
        return jax.random.normal(next(ks), shape, f32) * scale

    def gain(shape):
        return 1.0 + 0.05 * jax.random.normal(next(ks), shape, f32)

    n_pages = PAST_LEN // PAGE_SIZE
    n_phys = (DEC_BATCH * n_pages * 5) // 4
    x_prompt = nrm((BATCH, SEQ, D_MODEL), 1.0)
    x_sample = nrm((DEC_BATCH, DEC_SEQ, D_MODEL), 1.0)
    cache_k = nrm((DEPTH, n_phys, PAGE_SIZE, DA_HEADS, 2 * DA_QK_DIM), 1.0)
    cache_v = nrm((DEPTH, n_phys, PAGE_SIZE, DA_HEADS, DA_V_DIM), 1.0)
    state_pool = nrm((DEPTH, DEC_BATCH, POOL_BUF, POOL_WIDTH), 1.0)
    state_ret = nrm((DEPTH, DEC_BATCH, RET_HEADS, RET_QK_DIM, RET_V_DIM), 0.1)
    perm = jax.random.permutation(next(ks), n_phys)
    page_table = perm[:DEC_BATCH * n_pages].reshape(DEC_BATCH, n_pages).astype(jnp.int32)
    return {
        "x_prompt": x_prompt,
        "x_sample": x_sample,
        "cache_k": cache_k,
        "cache_v": cache_v,
        "state_pool": state_pool,
        "state_ret": state_ret,
        "page_table": page_table,
        "norm_mix_pre": gain((DEPTH, D_MODEL)),
        "norm_mix_post": gain((DEPTH, D_MODEL)),
        "norm_ffn_pre": gain((DEPTH, D_MODEL)),
        "norm_ffn_post": gain((DEPTH, D_MODEL)),
        "w_in": nrm((DEPTH, D_MODEL, IN_COLS), D_MODEL ** -0.5),
        "w_pool": nrm((DEPTH, N_POOL_GROUPS, POOL_GROUP, POOL_GROUP), POOL_GROUP ** -0.5),
        "pool_scale": gain((DEPTH, POOL_WIDTH)),
        "da_lambda_q1": nrm((DEPTH, DA_QK_DIM), 0.1),
        "da_lambda_k1": nrm((DEPTH, DA_QK_DIM), 0.1),
        "da_lambda_q2": nrm((DEPTH, DA_QK_DIM), 0.1),
        "da_lambda_k2": nrm((DEPTH, DA_QK_DIM), 0.1),
        "da_subln": gain((DEPTH, DA_V_DIM)),
        "ret_norm": gain((DEPTH, RET_WIDTH)),
        "w_br_a": nrm((DEPTH, POOL_WIDTH, D_MODEL), POOL_WIDTH ** -0.5),
        "w_br_b": nrm((DEPTH, DA_WIDTH, D_MODEL), DA_WIDTH ** -0.5),
        "w_br_c": nrm((DEPTH, RET_WIDTH, D_MODEL), RET_WIDTH ** -0.5),
        "w_o": nrm((DEPTH, D_MODEL, D_MODEL), D_MODEL ** -0.5),
        "w_ffn_in": nrm((DEPTH, D_MODEL, 2 * D_FF), D_MODEL ** -0.5),
        "w_ffn_out": nrm((DEPTH, D_FF, D_MODEL), D_FF ** -0.5),
    }


def reference(x_prompt, x_sample, cache_k, cache_v, state_pool, state_ret, page_table,
              norm_mix_pre, norm_mix_post, norm_ffn_pre, norm_ffn_post, w_in, w_pool, pool_scale,
              da_lambda_q1, da_lambda_k1, da_lambda_q2, da_lambda_k2, da_subln, ret_norm,
              w_br_a, w_br_b, w_br_c, w_o, w_ffn_in, w_ffn_out):
    bp, sp = x_prompt.shape[0], x_prompt.shape[1]
    bs, ts = x_sample.shape[0], x_sample.shape[1]
    past_len = page_table.shape[1] * PAGE_SIZE
    pos_p = jnp.arange(sp)
    pos_s = past_len + jnp.arange(ts)
    xp, xs = x_prompt, x_sample
    kp_l, vp_l, pp_l, rp_l, ks_l, vs_l, ps_l, rs_l = [], [], [], [], [], [], [], []
    for l in range(DEPTH):
        lam_init = 0.8 - 0.6 * math.exp(-0.3 * l)
        lam = (jnp.exp(jnp.sum(da_lambda_q1[l].astype(jnp.float32) * da_lambda_k1[l].astype(jnp.float32)))
               - jnp.exp(jnp.sum(da_lambda_q2[l].astype(jnp.float32) * da_lambda_k2[l].astype(jnp.float32)))
               + lam_init)
        lp = (norm_mix_pre[l], norm_mix_post[l], norm_ffn_pre[l], norm_ffn_post[l], w_in[l], w_pool[l],
              pool_scale[l], da_subln[l], ret_norm[l], w_br_a[l], w_br_b[l], w_br_c[l], w_o[l],
              w_ffn_in[l], w_ffn_out[l])
        xp, pool_p, k_p, v_p, s_p = trunk_layer(
            xp, pos_p, jnp.zeros((bp, POOL_BUF, POOL_WIDTH), xp.dtype),
            jnp.zeros((bp, RET_HEADS, RET_QK_DIM, RET_V_DIM), jnp.float32),
            RET_CHUNK, attend_prompt, lam, lam_init, lp)
        k_past = cache_k[l, page_table].reshape(bs, past_len, DA_HEADS, 2 * DA_QK_DIM)
        v_past = cache_v[l, page_table].reshape(bs, past_len, DA_HEADS, DA_V_DIM)
        attend_s = functools.partial(attend_sample, k_past=k_past, v_past=v_past)
        xs, pool_s, k_s, v_s, s_s = trunk_layer(
            xs, pos_s, state_pool[l], state_ret[l], ts, attend_s, lam, lam_init, lp)
        kp_l.append(k_p); vp_l.append(v_p); pp_l.append(pool_p); rp_l.append(s_p)
        ks_l.append(k_s); vs_l.append(v_s); ps_l.append(pool_s); rs_l.append(s_s)
    k_prompt = jnp.stack(kp_l); v_prompt = jnp.stack(vp_l)
    pool_prompt = jnp.stack(pp_l); ret_prompt = jnp.stack(rp_l)
    k_sample = jnp.stack(ks_l); v_sample = jnp.stack(vs_l)
    pool_sample = jnp.stack(ps_l); ret_sample = jnp.stack(rs_l)
    return (xp, xs, k_prompt, v_prompt, pool_prompt, ret_prompt, k_sample, v_sample, pool_sample, ret_sample)
```

```python
import functools
import math

import jax
import jax.numpy as jnp
from jax import lax
from jax.experimental import pallas as pl
from jax.experimental.pallas import tpu as pltpu

F32 = jnp.float32
BF16 = jnp.bfloat16
EPS = 1e-6
HEAD = 256
HALF = HEAD // 2
POOL_WINDOWS = (2, 4, 8, 16)
POOL_BUF = max(POOL_WINDOWS) - 1
POOL_HALO = 16
RET_CHUNK = 128
ROPE_BASE = 10000.0
SAMPLE_ROWS = 16
PAGES_PER_STEP = 8
VMEM_LIMIT = 56 * 1024 * 1024


def _params(sem):
    return pltpu.CompilerParams(dimension_semantics=sem, vmem_limit_bytes=VMEM_LIMIT)


def _tile(n, want):
    t = min(n, want)
    while n % t:
        t //= 2
    return t


def _rms_kernel(x_ref, g_ref, o_ref):
    x = x_ref[...]
    y = x * lax.rsqrt(jnp.mean(x * x, axis=-1, keepdims=True) + EPS)
    o_ref[...] = (y * g_ref[...]).astype(o_ref.dtype)


def rms_cast(x, g):
    m, d = x.shape
    tm = _tile(m, 512)
    return pl.pallas_call(
        _rms_kernel,
        out_shape=jax.ShapeDtypeStruct((m, d), BF16),
        grid=(m // tm,),
        in_specs=[pl.BlockSpec((tm, d), lambda i: (i, 0)),
                  pl.BlockSpec((1, d), lambda i: (0, 0))],
        out_specs=pl.BlockSpec((tm, d), lambda i: (i, 0)),
        compiler_params=_params(("parallel",)),
        name="rms_cast",
    )(x, g.reshape(1, d))


def _mm_kernel(a_ref, w_ref, o_ref):
    o_ref[...] = jnp.dot(a_ref[...], w_ref[...], preferred_element_type=F32).astype(o_ref.dtype)


def matmul(a, w, out_dtype=F32):
    m, k = a.shape
    n = w.shape[1]
    tm = _tile(m, 1024)
    tn = _tile(n, 512)
    return pl.pallas_call(
        _mm_kernel,
        out_shape=jax.ShapeDtypeStruct((m, n), out_dtype),
        grid=(m // tm, n // tn),
        in_specs=[pl.BlockSpec((tm, k), lambda i, j: (i, 0)),
                  pl.BlockSpec((k, tn), lambda i, j: (0, j))],
        out_specs=pl.BlockSpec((tm, tn), lambda i, j: (i, j)),
        compiler_params=_params(("parallel", "parallel")),
        name="in_proj",
    )(a, w)


def _pool_kernel(u_ref, halo_ref, w_ref, sc_ref, o_ref, ext_ref, *, ts, pos0, group):
    i = pl.program_id(1)
    halo = halo_ref[...]
    ext_ref[0:POOL_HALO, :] = jnp.where(i > 0, halo, jnp.zeros_like(halo))
    ext_ref[POOL_HALO:POOL_HALO + ts, :] = u_ref[...]
    pos = pos0 + i * ts + lax.broadcasted_iota(jnp.int32, (ts, group), 0)
    for g, win in enumerate(POOL_WINDOWS):
        cols = slice(g * group, (g + 1) * group)
        u = ext_ref[POOL_HALO:POOL_HALO + ts, cols]
        s = u
        for back in range(1, win):
            s = s + ext_ref[POOL_HALO - back:POOL_HALO - back + ts, cols]
        cnt = jnp.minimum(win, pos + 1).astype(F32)
        d = s / cnt - u
        y = jnp.dot(d.astype(BF16), w_ref[g], preferred_element_type=F32)
        o_ref[:, cols] = (y * sc_ref[:, cols]).astype(o_ref.dtype)


def pool_mix(u3, w_pool, pool_scale, pos0):
    b, s, _ = u3.shape
    n_groups, group, _ = w_pool.shape
    width = n_groups * group
    ts = _tile(s, 512)
    halo_blocks = ts // POOL_HALO
    return pl.pallas_call(
        functools.partial(_pool_kernel, ts=ts, pos0=pos0, group=group),
        out_shape=jax.ShapeDtypeStruct((b, s, width), BF16),
        grid=(b, s // ts),
        in_specs=[pl.BlockSpec((None, ts, width), lambda bi, i: (bi, i, 0)),
                  pl.BlockSpec((None, POOL_HALO, width),
                               lambda bi, i: (bi, jnp.maximum(i * halo_blocks - 1, 0), 0)),
                  pl.BlockSpec((n_groups, group, group), lambda bi, i: (0, 0, 0)),
                  pl.BlockSpec((1, width), lambda bi, i: (0, 0))],
        out_specs=pl.BlockSpec((None, ts, width), lambda bi, i: (bi, i, 0)),
        scratch_shapes=[pltpu.VMEM((POOL_HALO + ts, width), F32)],
        compiler_params=_params(("parallel", "arbitrary")),
        name="pool_mix",
    )(u3, u3, w_pool, pool_scale.reshape(1, width))


def _subln(o, g, post_scale):
    y = o * lax.rsqrt(jnp.mean(o * o, axis=-1, keepdims=True) + EPS)
    return y * g * post_scale


def _attn_prompt_kernel(lam_ref, q_ref, k_ref, v_ref, g_ref, o_ref,
                        m_ref, l_ref, acc_ref, *, tq, tk, nk, post_scale):
    i = pl.program_id(2)
    j = pl.program_id(3)
    scale = HALF ** -0.5

    @pl.when(j == 0)
    def _():
        m_ref[...] = jnp.full_like(m_ref, -jnp.inf)
        l_ref[...] = jnp.zeros_like(l_ref)
        acc_ref[...] = jnp.zeros_like(acc_ref)

    @pl.when(j * tk <= i * tq + tq - 1)
    def _():
        q = q_ref[...].astype(BF16)
        k = k_ref[...].astype(BF16)
        v = v_ref[...].astype(BF16)
        row = i * tq + lax.broadcasted_iota(jnp.int32, (tq, tk), 0)
        col = j * tk + lax.broadcasted_iota(jnp.int32, (tq, tk), 1)
        mask = col <= row
        for a in range(2):
            lanes = slice(a * HALF, (a + 1) * HALF)
            s = lax.dot_general(q[:, lanes], k[:, lanes], (((1,), (1,)), ((), ())),
                                preferred_element_type=F32) * scale
            s = jnp.where(mask, s, -jnp.inf)
            m_old = m_ref[a]
            m_new = jnp.maximum(m_old, jnp.max(s, axis=-1, keepdims=True))
            alpha = jnp.exp(m_old - m_new)
            p = jnp.exp(s - m_new)
            l_ref[a] = alpha * l_ref[a] + jnp.sum(p, axis=-1, keepdims=True)
            acc_ref[a] = alpha * acc_ref[a] + jnp.dot(p.astype(BF16), v, preferred_element_type=F32)
            m_ref[a] = m_new

    @pl.when(j == nk - 1)
    def _():
        o = acc_ref[0] / l_ref[0] - lam_ref[0, 0] * (acc_ref[1] / l_ref[1])
        o_ref[...] = _subln(o, g_ref[...], post_scale).astype(o_ref.dtype)


def attn_prompt(proj3, off_q, off_k, off_v, n_heads, lam, subln, post_scale):
    b, s, _ = proj3.shape
    tq = _tile(s, 512)
    tk = tq
    nk = s // tk
    cq, ck, cv = off_q // HEAD, off_k // HEAD, off_v // HEAD

    def kv_map(c0):
        return lambda bi, h, i, j: (bi, jnp.minimum(j, (i * tq + tq - 1) // tk), c0 + h)

    return pl.pallas_call(
        functools.partial(_attn_prompt_kernel, tq=tq, tk=tk, nk=nk, post_scale=post_scale),
        out_shape=jax.ShapeDtypeStruct((b, s, n_heads * HEAD), BF16),
        grid=(b, n_heads, s // tq, nk),
        in_specs=[pl.BlockSpec(memory_space=pltpu.SMEM),
                  pl.BlockSpec((None, tq, HEAD), lambda bi, h, i, j: (bi, i, cq + h)),
                  pl.BlockSpec((None, tk, HEAD), kv_map(ck)),
                  pl.BlockSpec((None, tk, HEAD), kv_map(cv)),
                  pl.BlockSpec((1, HEAD), lambda bi, h, i, j: (0, 0))],
        out_specs=pl.BlockSpec((None, tq, HEAD), lambda bi, h, i, j: (bi, i, h)),
        scratch_shapes=[pltpu.VMEM((2, tq, 1), F32), pltpu.VMEM((2, tq, 1), F32),
                        pltpu.VMEM((2, tq, HEAD), F32)],
        compiler_params=_params(("parallel", "parallel", "parallel", "arbitrary")),
        name="attn_prompt",
    )(lam.reshape(1, 1), proj3, proj3, proj3, subln.reshape(1, HEAD))


def _attn_sample_kernel(pt_ref, lam_ref, qz_ref, qs_ref, ks_ref, vs_ref, g_ref, *rest,
                        n_heads, page, n_steps, post_scale):
    pages = PAGES_PER_STEP
    k_refs = rest[:pages]
    v_refs = rest[pages:2 * pages]
    o_ref = rest[2 * pages]
    m_ref, l_ref, acc_ref = rest[2 * pages + 1:]
    step = pl.program_id(1)
    scale = HALF ** -0.5
    rows = 2 * n_heads
    keys = page * n_heads

    @pl.when(step == 0)
    def _():
        m_ref[...] = jnp.full_like(m_ref, -jnp.inf)
        l_ref[...] = jnp.zeros_like(l_ref)
        acc_ref[...] = jnp.zeros_like(acc_ref)

    qz = qz_ref[...]
    row_head = lax.broadcasted_iota(jnp.int32, (rows, keys), 0) % n_heads
    col_head = lax.broadcasted_iota(jnp.int32, (rows, keys), 1) % n_heads
    own = row_head == col_head
    scores = []
    for r in range(pages):
        kf = k_refs[r][...].reshape(keys, HEAD).astype(BF16)
        s = lax.dot_general(qz, kf, (((1,), (1,)), ((), ())), preferred_element_type=F32) * scale
        scores.append(jnp.where(own, s, -jnp.inf))
    m_old = m_ref[...]
    m_new = m_old
    for s in scores:
        m_new = jnp.maximum(m_new, jnp.max(s, axis=-1, keepdims=True))
    alpha = jnp.exp(m_old - m_new)
    l_new = alpha * l_ref[...]
    acc = alpha * acc_ref[...]
    for r in range(pages):
        p = jnp.exp(scores[r] - m_new)
        l_new = l_new + jnp.sum(p, axis=-1, keepdims=True)
        vf = v_refs[r][...].reshape(keys, HEAD).astype(BF16)
        acc = acc + jnp.dot(p.astype(BF16), vf, preferred_element_type=F32)
    m_ref[...] = m_new
    l_ref[...] = l_new
    acc_ref[...] = acc

    @pl.when(step == n_steps - 1)
    def _():
        qs = qs_ref[...].astype(BF16).astype(F32)
        ks = ks_ref[...].astype(BF16).astype(F32)
        vs = vs_ref[...].astype(BF16).astype(F32)
        prod = qs * ks
        outs = []
        for a in range(2):
            lanes = slice(a * HALF, (a + 1) * HALF)
            rws = slice(a * n_heads, (a + 1) * n_heads)
            s_self = jnp.sum(prod[:, lanes], axis=-1, keepdims=True) * scale
            m_a = m_ref[rws]
            m_fin = jnp.maximum(m_a, s_self)
            al = jnp.exp(m_a - m_fin)
            p_self = jnp.exp(s_self - m_fin)
            l_fin = al * l_ref[rws] + p_self
            acc_fin = al * acc_ref[rws] + p_self.astype(BF16).astype(F32) * vs
            outs.append(acc_fin / l_fin)
        o = outs[0] - lam_ref[0, 0] * outs[1]
        o_ref[...] = _subln(o, g_ref[...], post_scale).astype(o_ref.dtype)


def attn_sample(layer, cache_k, cache_v, page_table, q_s, k_s, v_s, lam, subln, post_scale):
    bs, n_heads, _ = q_s.shape
    n_pages = page_table.shape[1]
    page = cache_k.shape[2]
    pages = PAGES_PER_STEP
    assert n_pages % pages == 0
    n_steps = n_pages // pages
    zeros = jnp.zeros((bs, n_heads, HALF), F32)
    qz = jnp.concatenate([jnp.concatenate([q_s[..., :HALF], zeros], -1),
                          jnp.concatenate([zeros, q_s[..., HALF:]], -1)], axis=1).astype(BF16)

    def page_map(r):
        return lambda b, p, pt: (layer, pt[b * n_pages + p * pages + r], 0, 0, 0)

    row_spec = pl.BlockSpec((None, n_heads, HEAD), lambda b, p, pt: (b, 0, 0))
    page_block = (None, None, page, n_heads, HEAD)
    grid_spec = pltpu.PrefetchScalarGridSpec(
        num_scalar_prefetch=1,
        grid=(bs, n_steps),
        in_specs=[pl.BlockSpec(memory_space=pltpu.SMEM),
                  pl.BlockSpec((None, 2 * n_heads, HEAD), lambda b, p, pt: (b, 0, 0)),
                  row_spec, row_spec, row_spec,
                  pl.BlockSpec((1, HEAD), lambda b, p, pt: (0, 0))]
                 + [pl.BlockSpec(page_block, page_map(r)) for r in range(pages)]
                 + [pl.BlockSpec(page_block, page_map(r)) for r in range(pages)],
        out_specs=row_spec,
        scratch_shapes=[pltpu.VMEM((2 * n_heads, 1), F32), pltpu.VMEM((2 * n_heads, 1), F32),
                        pltpu.VMEM((2 * n_heads, HEAD), F32)],
    )
    return pl.pallas_call(
        functools.partial(_attn_sample_kernel, n_heads=n_heads, page=page, n_steps=n_steps,
                          post_scale=post_scale),
        out_shape=jax.ShapeDtypeStruct((bs, n_heads, HEAD), F32),
        grid_spec=grid_spec,
        compiler_params=_params(("parallel", "arbitrary")),
        name="attn_sample",
    )(page_table.reshape(-1), lam.reshape(1, 1), qz, q_s, k_s, v_s, subln.reshape(1, HEAD),
      *([cache_k] * pages), *([cache_v] * pages))


def _rope(x, cos, sin):
    x1, x2 = x[:, :HALF], x[:, HALF:]
    return jnp.concatenate([x1 * cos - x2 * sin, x1 * sin + x2 * cos], axis=-1)


def _retention_kernel(cdec_ref, q_ref, k_ref, v_ref, gate_ref, cos_ref, sin_ref, dec_ref,
                      qdec_ref, kdec_ref, s0_ref, g_ref, y_ref, sfin_ref, state_ref,
                      *, chunk, n_chunks, n_steps):
    h = pl.program_id(1)
    i = pl.program_id(2)

    @pl.when(i == 0)
    def _():
        state_ref[...] = s0_ref[...]

    c_dec = cdec_ref[h]
    dec = dec_ref[...]
    qdec = qdec_ref[...]
    kdec = kdec_ref[...]
    g = g_ref[...]
    for c in range(n_chunks):
        rows = slice(c * chunk, (c + 1) * chunk)
        cos = cos_ref[rows, :]
        sin = sin_ref[rows, :]
        q = _rope(q_ref[rows, :], cos, sin)
        k = _rope(k_ref[rows, :], cos, sin) * (HEAD ** -0.5)
        qb = q.astype(BF16)
        kb = k.astype(BF16)
        vb = v_ref[rows, :].astype(BF16)
        state = state_ref[...]
        sc = lax.dot_general(qb, kb, (((1,), (1,)), ((), ())), preferred_element_type=F32) * dec
        o = (jnp.dot(sc.astype(BF16), vb, preferred_element_type=F32)
             + jnp.dot(qb, state.astype(BF16), preferred_element_type=F32) * qdec)
        state_ref[...] = state * c_dec + lax.dot_general(
            (k * kdec).astype(BF16), vb, (((0,), (0,)), ((), ())), preferred_element_type=F32)
        y = o * lax.rsqrt(jnp.mean(o * o, axis=-1, keepdims=True) + EPS) * g
        gate = gate_ref[rows, :]
        y_ref[rows, :] = (gate * jax.nn.sigmoid(gate) * y).astype(y_ref.dtype)

    @pl.when(i == n_steps - 1)
    def _():
        sfin_ref[...] = state_ref[...]


def retention(proj3, off_q, off_k, off_v, off_g, n_heads, cos, sin, consts, s0, ret_norm, chunk):
    b, s, _ = proj3.shape
    dec, qdec, kdec, cdec = consts
    rows = _tile(s, 4 * chunk)
    n_chunks = rows // chunk
    n_steps = s // rows
    cq, ck, cv, cg = off_q // HEAD, off_k // HEAD, off_v // HEAD, off_g // HEAD

    def col_spec(c0):
        return pl.BlockSpec((None, rows, HEAD), lambda bi, h, i: (bi, i, c0 + h))

    head_spec = lambda shape: pl.BlockSpec((None,) + shape, lambda bi, h, i: (h, 0, 0))
    state_spec = pl.BlockSpec((None, None, HEAD, HEAD), lambda bi, h, i: (bi, h, 0, 0))
    return pl.pallas_call(
        functools.partial(_retention_kernel, chunk=chunk, n_chunks=n_chunks, n_steps=n_steps),
        out_shape=(jax.ShapeDtypeStruct((b, s, n_heads * HEAD), BF16),
                   jax.ShapeDtypeStruct((b, n_heads, HEAD, HEAD), F32)),
        grid=(b, n_heads, n_steps),
        in_specs=[pl.BlockSpec(memory_space=pltpu.SMEM),
                  col_spec(cq), col_spec(ck), col_spec(cv), col_spec(cg),
                  pl.BlockSpec((rows, HALF), lambda bi, h, i: (i, 0)),
                  pl.BlockSpec((rows, HALF), lambda bi, h, i: (i, 0)),
                  head_spec((chunk, chunk)), head_spec((chunk, HEAD)), head_spec((chunk, HEAD)),
                  state_spec,
                  pl.BlockSpec((1, HEAD), lambda bi, h, i: (0, h))],
        out_specs=(pl.BlockSpec((None, rows, HEAD), lambda bi, h, i: (bi, i, h)), state_spec),
        scratch_shapes=[pltpu.VMEM((HEAD, HEAD), F32)],
        compiler_params=_params(("parallel", "parallel", "arbitrary")),
        name="retention",
    )(cdec, proj3, proj3, proj3, proj3, cos, sin, dec, qdec, kdec, s0,
      ret_norm.reshape(1, n_heads * HEAD))


def retention_consts(n_heads, chunk, sample):
    log_g = jnp.log1p(-jnp.exp2(-5.0 - jnp.arange(n_heads, dtype=F32)))
    if sample:
        dec = jnp.broadcast_to(jnp.eye(chunk, dtype=F32), (n_heads, chunk, chunk))
        qdec = jnp.broadcast_to(jnp.exp(log_g)[:, None, None], (n_heads, chunk, HEAD))
        kdec = jnp.ones((n_heads, chunk, HEAD), F32)
        cdec = jnp.exp(log_g)
        return dec, qdec, kdec, cdec
    i = jnp.arange(chunk, dtype=F32)
    diff = i[:, None] - i[None, :]
    causal = diff >= 0
    dec = jnp.where(causal[None], jnp.exp(log_g[:, None, None] * jnp.where(causal, diff, 0.0)[None]), 0.0)
    qdec = jnp.broadcast_to(jnp.exp(log_g[:, None] * (i[None, :] + 1.0))[:, :, None],
                            (n_heads, chunk, HEAD))
    kdec = jnp.broadcast_to(jnp.exp(log_g[:, None] * (chunk - 1.0 - i[None, :]))[:, :, None],
                            (n_heads, chunk, HEAD))
    cdec = jnp.exp(log_g * chunk)
    return dec, qdec, kdec, cdec


def rope_tables(pos):
    inv = ROPE_BASE ** (-jnp.arange(HALF, dtype=F32) / HALF)
    ang = pos.astype(F32)[:, None] * inv[None, :]
    return jnp.cos(ang), jnp.sin(ang)


def _merge_kernel(ya_ref, yb_ref, yc_ref, wa_ref, wb_ref, wc_ref, ga_ref, gb_ref, gc_ref, o_ref):
    acc = jax.nn.sigmoid(ga_ref[...]) * jnp.dot(ya_ref[...], wa_ref[...], preferred_element_type=F32)
    acc += jax.nn.sigmoid(gb_ref[...]) * jnp.dot(yb_ref[...], wb_ref[...], preferred_element_type=F32)
    acc += jax.nn.sigmoid(gc_ref[...]) * jnp.dot(yc_ref[...], wc_ref[...], preferred_element_type=F32)
    o_ref[...] = acc.astype(o_ref.dtype)


def merge(ya, yb, yc, wa, wb, wc, proj, off_gates):
    m, d = ya.shape
    tm = _tile(m, 512)
    tn = _tile(d, 512)
    c0 = off_gates // tn
    nb = d // tn
    y_spec = pl.BlockSpec((tm, d), lambda i, j: (i, 0))
    w_spec = pl.BlockSpec((d, tn), lambda i, j: (0, j))
    gate_spec = lambda t: pl.BlockSpec((tm, tn), lambda i, j: (i, c0 + t * nb + j))
    return pl.pallas_call(
        _merge_kernel,
        out_shape=jax.ShapeDtypeStruct((m, d), BF16),
        grid=(m // tm, nb),
        in_specs=[y_spec, y_spec, y_spec, w_spec, w_spec, w_spec,
                  gate_spec(0), gate_spec(1), gate_spec(2)],
        out_specs=pl.BlockSpec((tm, tn), lambda i, j: (i, j)),
        compiler_params=_params(("parallel", "parallel")),
        name="merge",
    )(ya, yb, yc, wa, wb, wc, proj, proj, proj)


def _mm_norm_res_kernel(a_ref, w_ref, x_ref, gpost_ref, gnext_ref, xo_ref, ho_ref, acc_ref, *, nk):
    k = pl.program_id(1)

    @pl.when(k == 0)
    def _():
        acc_ref[...] = jnp.zeros_like(acc_ref)

    acc_ref[...] += jnp.dot(a_ref[...], w_ref[...], preferred_element_type=F32)

    @pl.when(k == nk - 1)
    def _():
        y = acc_ref[...]
        y = y * lax.rsqrt(jnp.mean(y * y, axis=-1, keepdims=True) + EPS) * gpost_ref[...]
        x = x_ref[...] + y
        xo_ref[...] = x
        h = x * lax.rsqrt(jnp.mean(x * x, axis=-1, keepdims=True) + EPS) * gnext_ref[...]
        ho_ref[...] = h.astype(ho_ref.dtype)


def mm_norm_res(a, w, x, g_post, g_next):
    m, kdim = a.shape
    d = w.shape[1]
    tm = _tile(m, 512)
    tk = _tile(kdim, 512)
    nk = kdim // tk
    row_spec = pl.BlockSpec((tm, d), lambda i, k: (i, 0))
    vec_spec = pl.BlockSpec((1, d), lambda i, k: (0, 0))
    return pl.pallas_call(
        functools.partial(_mm_norm_res_kernel, nk=nk),
        out_shape=(jax.ShapeDtypeStruct((m, d), F32), jax.ShapeDtypeStruct((m, d), BF16)),
        grid=(m // tm, nk),
        in_specs=[pl.BlockSpec((tm, tk), lambda i, k: (i, k)),
                  pl.BlockSpec((tk, d), lambda i, k: (k, 0)),
                  row_spec, vec_spec, vec_spec],
        out_specs=(row_spec, row_spec),
        scratch_shapes=[pltpu.VMEM((tm, d), F32)],
        compiler_params=_params(("parallel", "arbitrary")),
        name="mm_norm_res",
    )(a, w, x, g_post.reshape(1, d), g_next.reshape(1, d))


def _ffn_in_kernel(h_ref, wa_ref, wb_ref, o_ref):
    h = h_ref[...]
    a = jnp.dot(h, wa_ref[...], preferred_element_type=F32)
    b = jnp.dot(h, wb_ref[...], preferred_element_type=F32)
    o_ref[...] = (a * jax.nn.sigmoid(a) * b).astype(o_ref.dtype)


def ffn_in(h, w):
    m, d = h.shape
    f = w.shape[1] // 2
    tm = _tile(m, 1024)
    tn = 512 if f % 512 == 0 else 256
    nb = f // tn
    return pl.pallas_call(
        _ffn_in_kernel,
        out_shape=jax.ShapeDtypeStruct((m, f), BF16),
        grid=(m // tm, nb),
        in_specs=[pl.BlockSpec((tm, d), lambda i, j: (i, 0)),
                  pl.BlockSpec((d, tn), lambda i, j: (0, j)),
                  pl.BlockSpec((d, tn), lambda i, j: (0, nb + j))],
        out_specs=pl.BlockSpec((tm, tn), lambda i, j: (i, j)),
        compiler_params=_params(("parallel", "parallel")),
        name="ffn_in",
    )(h, w, w)


def _pad_rows(a, rows):
    return jnp.concatenate([a, jnp.zeros((rows - a.shape[0],) + a.shape[1:], a.dtype)], axis=0)


def kernel(x_prompt, x_sample, cache_k, cache_v, state_pool, state_ret, page_table, norm_mix_pre, norm_mix_post, norm_ffn_pre, norm_ffn_post, w_in, w_pool, pool_scale, da_lambda_q1, da_lambda_k1, da_lambda_q2, da_lambda_k2, da_subln, ret_norm, w_br_a, w_br_b, w_br_c, w_o, w_ffn_in, w_ffn_out):
    depth = w_in.shape[0]
    bp, sp, d = x_prompt.shape
    bs, ts_new, _ = x_sample.shape
    assert ts_new == 1 and bs <= SAMPLE_ROWS
    n_heads = d // HEAD
    page = cache_k.shape[2]
    past_len = page_table.shape[1] * page
    off_u, off_dq, off_dk, off_dv, off_rq, off_rk, off_rv, off_rg, off_gates = (
        0, d, 2 * d, 3 * d, 4 * d, 5 * d, 6 * d, 7 * d, 8 * d)
    mp = bp * sp

    cos_p, sin_p = rope_tables(jnp.arange(sp))
    cos_s, sin_s = rope_tables(jnp.full((SAMPLE_ROWS,), past_len))
    ret_chunk = min(RET_CHUNK, sp)
    consts_p = retention_consts(n_heads, ret_chunk, sample=False)
    consts_s = retention_consts(n_heads, SAMPLE_ROWS, sample=True)
    zero_state = jnp.zeros((bp, n_heads, HEAD, HEAD), F32)

    xp = x_prompt.reshape(mp, d)
    xs = _pad_rows(x_sample.reshape(bs, d), SAMPLE_ROWS)
    hp = rms_cast(xp, norm_mix_pre[0])
    hs = rms_cast(xs, norm_mix_pre[0])
    outs = [[] for _ in range(8)]
    for l in range(depth):
        lam_init = 0.8 - 0.6 * math.exp(-0.3 * l)
        lam = (jnp.exp(jnp.sum(da_lambda_q1[l] * da_lambda_k1[l]))
               - jnp.exp(jnp.sum(da_lambda_q2[l] * da_lambda_k2[l])) + lam_init)
        post_scale = 1.0 - lam_init
        w_in_l = w_in[l].astype(BF16)
        w_pool_l = w_pool[l].astype(BF16)
        wa, wb, wc = w_br_a[l].astype(BF16), w_br_b[l].astype(BF16), w_br_c[l].astype(BF16)
        wo = w_o[l].astype(BF16)
        wf_in = w_ffn_in[l].astype(BF16)
        wf_out = w_ffn_out[l].astype(BF16)
        g_next = norm_mix_pre[l + 1] if l + 1 < depth else jnp.ones((d,), F32)

        proj = matmul(hp, w_in_l)
        proj3 = proj.reshape(bp, sp, -1)
        ya = pool_mix(proj3, w_pool_l, pool_scale[l], pos0=0).reshape(mp, d)
        yb = attn_prompt(proj3, off_dq, off_dk, off_dv, n_heads, lam, da_subln[l],
                         post_scale).reshape(mp, d)
        yc, s_p = retention(proj3, off_rq, off_rk, off_rv, off_rg, n_heads, cos_p, sin_p, consts_p,
                            zero_state, ret_norm[l], ret_chunk)
        merged = merge(ya, yb, yc.reshape(mp, d), wa, wb, wc, proj, off_gates)
        xp_mid, h_ffn = mm_norm_res(merged, wo, xp, norm_mix_post[l], norm_ffn_pre[l])
        act = ffn_in(h_ffn, wf_in)
        xp, hp = mm_norm_res(act, wf_out, xp_mid, norm_ffn_post[l], g_next)
        outs[0].append(proj3[:, :, off_dk:off_dk + d].reshape(bp, sp, n_heads, HEAD))
        outs[1].append(proj3[:, :, off_dv:off_dv + d].reshape(bp, sp, n_heads, HEAD))
        outs[2].append(proj3[:, sp - POOL_BUF:, off_u:off_u + d])
        outs[3].append(s_p)

        proj_s = matmul(hs, w_in_l)
        rows_s = proj_s[:bs]
        u_s = rows_s[:, off_u:off_u + d]
        pool_new = jnp.concatenate([state_pool[l], u_s[:, None, :]], axis=1)
        ya_s = pool_mix(pool_new, w_pool_l, pool_scale[l], pos0=past_len - POOL_BUF)[:, POOL_BUF]
        q_s = rows_s[:, off_dq:off_dq + d].reshape(bs, n_heads, HEAD)
        k_s = rows_s[:, off_dk:off_dk + d].reshape(bs, n_heads, HEAD)
        v_s = rows_s[:, off_dv:off_dv + d].reshape(bs, n_heads, HEAD)
        yb_s = attn_sample(l, cache_k, cache_v, page_table, q_s, k_s, v_s, lam, da_subln[l],
                           post_scale).reshape(bs, d).astype(BF16)
        proj_s3 = jnp.concatenate(
            [rows_s[:, None, :], jnp.zeros((bs, SAMPLE_ROWS - 1, rows_s.shape[1]), F32)], axis=1)
        yc_s3, s_s = retention(proj_s3, off_rq, off_rk, off_rv, off_rg, n_heads, cos_s, sin_s,
                               consts_s, state_ret[l], ret_norm[l], SAMPLE_ROWS)
        merged_s = merge(_pad_rows(ya_s, SAMPLE_ROWS), _pad_rows(yb_s, SAMPLE_ROWS),
                         _pad_rows(yc_s3[:, 0], SAMPLE_ROWS), wa, wb, wc, proj_s, off_gates)
        xs_mid, h_ffn_s = mm_norm_res(merged_s, wo, xs, norm_mix_post[l], norm_ffn_pre[l])
        act_s = ffn_in(h_ffn_s, wf_in)
        xs, hs = mm_norm_res(act_s, wf_out, xs_mid, norm_ffn_post[l], g_next)
        outs[4].append(k_s.reshape(bs, 1, n_heads, HEAD))
        outs[5].append(v_s.reshape(bs, 1, n_heads, HEAD))
        outs[6].append(pool_new[:, 1:])
        outs[7].append(s_s)

    k_p, v_p, pool_p, ret_p, k_sm, v_sm, pool_sm, ret_sm = [jnp.stack(o) for o in outs]
    return (xp.reshape(bp, sp, d), xs[:bs].reshape(bs, 1, d), k_p, v_p, pool_p, ret_p,
            k_sm, v_sm, pool_sm, ret_sm)
```
